```python
import math
import jax, jax.numpy as jnp
from jax import lax
import numpy as np

D_MODEL = 1024
BATCH = 8
SEQ = 8192
DEPTH = 1

DIFF_HEADS = 4
DIFF_HEAD_DIM = 64
DIFF_VAL_DIM = 2 * DIFF_HEAD_DIM
RET_HEADS = 4
RET_KEY_DIM = 64
RET_VAL_DIM = 128
D_FF = 256 * ((8 * D_MODEL // 3 + 255) // 256)
CONV_WIDTH = 3
Q_BLOCK = 128
RET_CHUNK = 128
NORM_EPS = 1e-6
SUBLN_EPS = 1e-5

SPLIT_SIZES = (
    DIFF_HEADS * DIFF_HEAD_DIM,
    DIFF_HEADS * DIFF_HEAD_DIM,
    DIFF_HEADS * DIFF_HEAD_DIM,
    DIFF_HEADS * DIFF_HEAD_DIM,
    DIFF_HEADS * DIFF_VAL_DIM,
    RET_HEADS * RET_KEY_DIM,
    RET_HEADS * RET_KEY_DIM,
    RET_HEADS * RET_VAL_DIM,
    RET_HEADS * RET_VAL_DIM,
    D_MODEL,
    D_MODEL,
)
D_IN = sum(SPLIT_SIZES)

kernel_name = 'hybrid_diffattn_retention_convffn'


def _rms(x, eps):
    xf = x.astype(jnp.float32)
    return xf * lax.rsqrt(jnp.mean(xf * xf, axis=-1, keepdims=True) + eps)


def rms_norm(x, g, eps=NORM_EPS):
    return (_rms(x, eps) * g.astype(jnp.float32)).astype(x.dtype)


def diff_attention(q1, q2, k1, k2, v, lam):
    B, S, H, d = q1.shape
    dv = v.shape[-1]
    nb = S // Q_BLOCK
    scale = d ** -0.5
    slopes = 2.0 ** (-8.0 * (jnp.arange(H, dtype=jnp.float32) + 1.0) / H)
    tk = jnp.arange(S)
    k1f, k2f, vf = k1.astype(jnp.float32), k2.astype(jnp.float32), v.astype(jnp.float32)

    def block(args):
        q1b, q2b, i = args
        tq = i * Q_BLOCK + jnp.arange(Q_BLOCK)
        dist = (tq[:, None] - tk[None, :]).astype(jnp.float32)
        bias = jnp.where(dist >= 0, -slopes[:, None, None] * dist, -jnp.inf)
        s1 = jnp.einsum('bqhd,bkhd->bhqk', q1b.astype(jnp.float32), k1f) * scale + bias
        s2 = jnp.einsum('bqhd,bkhd->bhqk', q2b.astype(jnp.float32), k2f) * scale + bias
        a = jax.nn.softmax(s1, axis=-1) - lam * jax.nn.softmax(s2, axis=-1)
        return jnp.einsum('bhqk,bkhe->bqhe', a, vf)

    q1b = q1.reshape(B, nb, Q_BLOCK, H, d).transpose(1, 0, 2, 3, 4)
    q2b = q2.reshape(B, nb, Q_BLOCK, H, d).transpose(1, 0, 2, 3, 4)
    out = lax.map(block, (q1b, q2b, jnp.arange(nb)))
    return out.transpose(1, 0, 2, 3, 4).reshape(B, S, H, dv)


def retention(q, k, v):
    B, S, H, dk = q.shape
    dv = v.shape[-1]
    C = RET_CHUNK
    nc = S // C
    gammas = 1.0 - 2.0 ** (-5.0 - jnp.arange(H, dtype=jnp.float32))
    log_g = jnp.log(gammas)
    idx = jnp.arange(C, dtype=jnp.float32)
    rel = idx[:, None] - idx[None, :]
    inner_decay = jnp.where(rel >= 0, jnp.exp(log_g[:, None, None] * jnp.maximum(rel, 0.0)), 0.0)
    q_decay = jnp.exp(log_g[:, None] * (idx + 1.0))[..., None]
    k_decay = jnp.exp(log_g[:, None] * (C - 1.0 - idx))[..., None]
    chunk_decay = jnp.exp(log_g * C)[None, :, None, None]

    def to_chunks(t):
        return t.astype(jnp.float32).reshape(B, nc, C, H, -1).transpose(1, 0, 3, 2, 4)

    qc, kc, vc = to_chunks(q), to_chunks(k) * (dk ** -0.5), to_chunks(v)

    def step(state, inp):
        qi, ki, vi = inp
        scores = jnp.einsum('bhqd,bhkd->bhqk', qi, ki) * inner_decay
        o = jnp.einsum('bhqk,bhke->bhqe', scores, vi) + jnp.einsum('bhqd,bhde->bhqe', qi * q_decay, state)
        state = state * chunk_decay + jnp.einsum('bhkd,bhke->bhde', ki * k_decay, vi)
        return state, o

    s0 = jnp.zeros((B, H, dk, dv), jnp.float32)
    _, o = lax.scan(step, s0, (qc, kc, vc))
    return o.transpose(1, 0, 3, 2, 4).reshape(B, S, H, dv)


def causal_dwconv(u, w, b):
    K = w.shape[0]
    S = u.shape[1]
    up = jnp.pad(u, ((0, 0), (K - 1, 0), (0, 0)))
    y = b + up[:, K - 1:K - 1 + S] * w[K - 1]
    for j in range(K - 1):
        y = y + up[:, j:j + S] * w[j]
    return y


def setup_inputs(seed: int = 0) -> dict:
    key = jax.random.key(seed)
    ks = jax.random.split(key, 20)
    L, D, F = DEPTH, D_MODEL, D_FF
    nrm = lambda k, shape, fan: jax.random.normal(k, shape, jnp.float32) * (fan ** -0.5)
    gain = lambda k, shape: 1.0 + 0.02 * jax.random.normal(k, shape, jnp.float32)
    return {
        'x': jax.random.normal(ks[0], (BATCH, SEQ, D), jnp.float32),
        'g_mix': gain(ks[1], (L, D)),
        'w_in': nrm(ks[2], (L, D, D_IN), D),
        'lam_q1': 0.1 * jax.random.normal(ks[3], (L, DIFF_HEAD_DIM), jnp.float32),
        'lam_k1': 0.1 * jax.random.normal(ks[4], (L, DIFF_HEAD_DIM), jnp.float32),
        'lam_q2': 0.1 * jax.random.normal(ks[5], (L, DIFF_HEAD_DIM), jnp.float32),
        'lam_k2': 0.1 * jax.random.normal(ks[6], (L, DIFF_HEAD_DIM), jnp.float32),
        'g_diff_sub': gain(ks[7], (L, DIFF_VAL_DIM)),
        'w_diff_proj': nrm(ks[8], (L, DIFF_HEADS * DIFF_VAL_DIM, D), DIFF_HEADS * DIFF_VAL_DIM),
        'w_ret_proj': nrm(ks[9], (L, RET_HEADS * RET_VAL_DIM, D), RET_HEADS * RET_VAL_DIM),
        'w_out': nrm(ks[10], (L, D, D), D),
        'g_ffn': gain(ks[11], (L, D)),
        'w_up': nrm(ks[12], (L, D, 2 * F), D),
        'conv_w': nrm(ks[13], (L, CONV_WIDTH, 2 * F), CONV_WIDTH),
        'conv_b': 0.01 * jax.random.normal(ks[14], (L, 2 * F), jnp.float32),
        'w_down': nrm(ks[15], (L, F, D), F),
        'g_final': gain(ks[16], (D,)),
    }


def reference(x, g_mix, w_in, lam_q1, lam_k1, lam_q2, lam_k2, g_diff_sub, w_diff_proj, w_ret_proj,
              w_out, g_ffn, w_up, conv_w, conv_b, w_down, g_final):
    B, S, _ = x.shape
    dt = x.dtype
    split_idx = np.cumsum(SPLIT_SIZES)[:-1].tolist()
    for l in range(DEPTH):
        h = rms_norm(x, g_mix[l])
        proj = h @ w_in[l]
        q1, q2, k1, k2, va, qr, kr, vr, gr, gate_a, gate_r = jnp.split(proj, split_idx, axis=-1)

        lam_init = 0.8 - 0.6 * math.exp(-0.3 * l)
        lam = (jnp.exp(jnp.sum(lam_q1[l].astype(jnp.float32) * lam_k1[l].astype(jnp.float32)))
               - jnp.exp(jnp.sum(lam_q2[l].astype(jnp.float32) * lam_k2[l].astype(jnp.float32))) + lam_init)
        a = diff_attention(q1.reshape(B, S, DIFF_HEADS, DIFF_HEAD_DIM), q2.reshape(B, S, DIFF_HEADS, DIFF_HEAD_DIM),
                           k1.reshape(B, S, DIFF_HEADS, DIFF_HEAD_DIM), k2.reshape(B, S, DIFF_HEADS, DIFF_HEAD_DIM),
                           va.reshape(B, S, DIFF_HEADS, DIFF_VAL_DIM), lam)
        a = (_rms(a, SUBLN_EPS) * g_diff_sub[l].astype(jnp.float32) * (1.0 - lam_init)).reshape(B, S, -1).astype(dt)

        r = retention(qr.reshape(B, S, RET_HEADS, RET_KEY_DIM), kr.reshape(B, S, RET_HEADS, RET_KEY_DIM),
                      vr.reshape(B, S, RET_HEADS, RET_VAL_DIM))
        r = (_rms(r, SUBLN_EPS).reshape(B, S, -1) * jax.nn.silu(gr.astype(jnp.float32))).astype(dt)

        merged = jax.nn.sigmoid(gate_a) * (a @ w_diff_proj[l]) + jax.nn.sigmoid(gate_r) * (r @ w_ret_proj[l])
        x = x + merged @ w_out[l]

        h2 = rms_norm(x, g_ffn[l])
        z = causal_dwconv(h2 @ w_up[l], conv_w[l], conv_b[l])
        zg, zu = jnp.split(z, 2, axis=-1)
        x = x + (jax.nn.silu(zg) * zu) @ w_down[l]
    return rms_norm(x, g_final)
```

```python
import functools
import math

import jax
import jax.numpy as jnp
from jax import lax
from jax.experimental import pallas as pl
from jax.experimental.pallas import tpu as pltpu

D_MODEL = 1024
N_HEADS = 4
HEAD_QK = 64
HEAD_V = 128
D_FF = 2816
CONV_WIDTH = 3
NORM_EPS = 1e-6
SUBLN_EPS = 1e-5

LANES = 128
SUBLANES = 8
VMEM_LIMIT = 56 * 1024 * 1024

ROW_TILE = 512
RET_CHUNK = 256
FF_CHUNK = 256

_OFF_Q1, _OFF_Q2, _OFF_K1, _OFF_K2, _OFF_VA = 0, 256, 512, 768, 1024
_OFF_QR, _OFF_KR, _OFF_VR, _OFF_GATES = 1536, 1792, 2048, 2560
HV = N_HEADS * HEAD_V
_COL_QQ, _COL_KK, _COL_QR, _COL_KR, _COL_VR = 0, 1, 2, 3, 4
N_PROJ_BLOCKS = 5

NEG_BIG = -1e30
BF16 = jnp.bfloat16
F32 = jnp.float32


def _rms(x, eps):
    return x * lax.rsqrt(jnp.mean(x * x, axis=-1, keepdims=True) + eps)


def _dot(a, b):
    return jnp.dot(a, b, preferred_element_type=F32)


def _dot_nt(a, b):
    return lax.dot_general(a, b, (((1,), (1,)), ((), ())), preferred_element_type=F32)


def _dot_tn(a, b):
    return lax.dot_general(a, b, (((0,), (0,)), ((), ())), preferred_element_type=F32)


def _in_proj_kernel(x_ref, g_ref, w_ref, o_ref, vt_ref):
    h = (_rms(x_ref[0], NORM_EPS) * g_ref[...]).astype(BF16)
    for c in range(N_PROJ_BLOCKS):
        cols = slice(c * HV, (c + 1) * HV)
        o_ref[0, :, cols] = _dot(h, w_ref[:, cols]).astype(BF16)
    va = _dot(h, w_ref[:, N_PROJ_BLOCKS * HV:])
    vt_ref[0, 0] = va.T.astype(BF16)


def _in_proj(x, g, w):
    B, S, D = x.shape
    n_tiles = S // ROW_TILE
    n_cols = N_PROJ_BLOCKS * HV
    return pl.pallas_call(
        _in_proj_kernel,
        grid=(B, n_tiles),
        in_specs=[
            pl.BlockSpec((1, ROW_TILE, D), lambda b, i: (b, i, 0)),
            pl.BlockSpec((1, D), lambda b, i: (0, 0)),
            pl.BlockSpec((D, n_cols + HV), lambda b, i: (0, 0)),
        ],
        out_specs=[
            pl.BlockSpec((1, ROW_TILE, n_cols), lambda b, i: (b, i, 0)),
            pl.BlockSpec((1, 1, HV, ROW_TILE), lambda b, i: (b, i, 0, 0)),
        ],
        out_shape=[
            jax.ShapeDtypeStruct((B, S, n_cols), BF16),
            jax.ShapeDtypeStruct((B, n_tiles, HV, ROW_TILE), BF16),
        ],
        compiler_params=pltpu.CompilerParams(
            dimension_semantics=("arbitrary", "arbitrary"), vmem_limit_bytes=VMEM_LIMIT),
        name="in_proj",
    )(x, g, w)


def _diff_attn_kernel(lq1_ref, lk1_ref, lq2_ref, lk2_ref, gsub_ref, qq_ref, kk_ref, vt_ref, o_ref,
                      bias_ref, acc1_ref, acc2_ref, m1_ref, l1_ref, m2_ref, l2_ref, *, lam_init):
    T = ROW_TILE
    h = pl.program_id(1)
    qi = pl.program_id(2)
    slope = jnp.where(h == 0, 2.0 ** -2, jnp.where(h == 1, 2.0 ** -4, jnp.where(h == 2, 2.0 ** -6, 2.0 ** -8)))
    slope = slope.astype(F32)

    key_idx = lax.broadcasted_iota(jnp.int32, (T, T), 0)
    qry_idx = lax.broadcasted_iota(jnp.int32, (T, T), 1)
    bias_ref[...] = key_idx.astype(F32) * slope

    qq = qq_ref[0] * jnp.asarray(HEAD_QK ** -0.5, BF16)
    lane = lax.broadcasted_iota(jnp.int32, qq.shape, 1)
    zero = jnp.zeros_like(qq)
    q1 = jnp.where(lane < HEAD_QK, qq, zero)
    q2 = jnp.where(lane >= HEAD_QK, qq, zero)

    for ref in (m1_ref, m2_ref):
        ref[...] = jnp.full(ref.shape, NEG_BIG, F32)
    for ref in (l1_ref, l2_ref, acc1_ref, acc2_ref):
        ref[...] = jnp.zeros(ref.shape, F32)

    def block(j, masked):
        start = pl.multiple_of(j * T, T)
        kk = kk_ref[0, pl.ds(start, T), :]
        vt = vt_ref[0, j]
        offset = slope * ((j - qi) * T).astype(F32)
        for q, m_ref, l_ref, acc_ref in ((q1, m1_ref, l1_ref, acc1_ref), (q2, m2_ref, l2_ref, acc2_ref)):
            s = _dot_nt(kk, q) + bias_ref[...]
            if masked:
                s = jnp.where(key_idx <= qry_idx, s, NEG_BIG)
            m_old = m_ref[...]
            m_new = jnp.maximum(m_old, jnp.max(s, axis=0, keepdims=True) + offset)
            alpha = jnp.exp(m_old - m_new)
            p = jnp.exp(s - (m_new - offset))
            l_ref[...] = alpha * l_ref[...] + jnp.sum(p, axis=0, keepdims=True)
            acc_ref[...] = alpha * acc_ref[...] + _dot(vt, p.astype(BF16))
            m_ref[...] = m_new

    def body(j, carry):
        block(j, masked=False)
        return carry

    lax.fori_loop(0, qi, body, 0)
    block(qi, masked=True)

    lam = (jnp.exp(jnp.sum(lq1_ref[...] * lk1_ref[...], axis=-1, keepdims=True))
           - jnp.exp(jnp.sum(lq2_ref[...] * lk2_ref[...], axis=-1, keepdims=True)) + lam_init)
    a = acc1_ref[...] * (1.0 / l1_ref[...]) - lam * (acc2_ref[...] * (1.0 / l2_ref[...]))
    a = a * lax.rsqrt(jnp.mean(a * a, axis=0, keepdims=True) + SUBLN_EPS)
    o_ref[0] = (a.T * (gsub_ref[...] * (1.0 - lam_init))).astype(BF16)


def _diff_attn(proj, vt, lam_q1, lam_k1, lam_q2, lam_k2, g_sub, lam_init):
    B, S, _ = proj.shape
    T = ROW_TILE
    n_tiles = S // T
    small = lambda n: pl.BlockSpec((1, n), lambda b, h, i: (0, 0))
    return pl.pallas_call(
        functools.partial(_diff_attn_kernel, lam_init=lam_init),
        grid=(B, N_HEADS, n_tiles),
        in_specs=[
            small(HEAD_QK), small(HEAD_QK), small(HEAD_QK), small(HEAD_QK), small(HEAD_V),
            pl.BlockSpec((1, T, LANES), lambda b, h, i: (b, i, _COL_QQ * N_HEADS + h)),
            pl.BlockSpec((1, S, LANES), lambda b, h, i: (b, 0, _COL_KK * N_HEADS + h)),
            pl.BlockSpec((1, n_tiles, HEAD_V, T), lambda b, h, i: (b, 0, h, 0)),
        ],
        out_specs=pl.BlockSpec((1, T, HEAD_V), lambda b, h, i: (b, i, h)),
        out_shape=jax.ShapeDtypeStruct((B, S, HV), BF16),
        scratch_shapes=[
            pltpu.VMEM((T, T), F32),
            pltpu.VMEM((HEAD_V, T), F32), pltpu.VMEM((HEAD_V, T), F32),
            pltpu.VMEM((1, T), F32), pltpu.VMEM((1, T), F32), pltpu.VMEM((1, T), F32), pltpu.VMEM((1, T), F32),
        ],
        compiler_params=pltpu.CompilerParams(
            dimension_semantics=("arbitrary", "arbitrary", "arbitrary"), vmem_limit_bytes=VMEM_LIMIT),
        name="diff_attn",
    )(lam_q1, lam_k1, lam_q2, lam_k2, g_sub, proj, proj, vt)


def _retention_kernel(q_ref, k_ref, v_ref, o_ref, state_ref, decay_ref, qdec_ref, kdec_ref):
    C = RET_CHUNK
    log_g = [math.log(1.0 - 2.0 ** (-5.0 - h)) for h in range(N_HEADS)]

    @pl.when((pl.program_id(0) == 0) & (pl.program_id(1) == 0))
    def _():
        row = lax.broadcasted_iota(jnp.int32, (C, C), 0)
        col = lax.broadcasted_iota(jnp.int32, (C, C), 1)
        rel = (row - col).astype(F32)
        pos = lax.broadcasted_iota(jnp.int32, (C, LANES), 0).astype(F32)
        for h in range(N_HEADS):
            decay_ref[h] = jnp.where(rel >= 0, jnp.exp(log_g[h] * jnp.maximum(rel, 0.0)), 0.0)
            qdec_ref[h] = jnp.exp(log_g[h] * (pos + 1.0))
            kdec_ref[h] = jnp.exp(log_g[h] * (C - 1.0 - pos))

    @pl.when(pl.program_id(1) == 0)
    def _():
        state_ref[...] = jnp.zeros(state_ref.shape, F32)

    for h in range(N_HEADS):
        cols = slice(h * LANES, (h + 1) * LANES)
        q = q_ref[0, :, cols]
        k = k_ref[0, :, cols] * jnp.asarray(HEAD_QK ** -0.5, BF16)
        v = v_ref[0, :, cols]
        scores = _dot_nt(q, k) * decay_ref[h]
        state = state_ref[h]
        o = _dot(scores.astype(BF16), v) + _dot((q.astype(F32) * qdec_ref[h]).astype(BF16), state.astype(BF16))
        state_ref[h] = state * math.exp(log_g[h] * C) + _dot_tn((k.astype(F32) * kdec_ref[h]).astype(BF16), v)
        o_ref[0, :, cols] = _rms(o, SUBLN_EPS)


def _retention(proj):
    B, S, _ = proj.shape
    C = RET_CHUNK
    col = lambda c: pl.BlockSpec((1, C, HV), lambda b, i: (b, i, c))
    return pl.pallas_call(
        _retention_kernel,
        grid=(B, S // C),
        in_specs=[col(_COL_QR), col(_COL_KR), col(_COL_VR)],
        out_specs=pl.BlockSpec((1, C, HV), lambda b, i: (b, i, 0)),
        out_shape=jax.ShapeDtypeStruct((B, S, HV), F32),
        scratch_shapes=[
            pltpu.VMEM((N_HEADS, LANES, HEAD_V), F32),
            pltpu.VMEM((N_HEADS, C, C), F32),
            pltpu.VMEM((N_HEADS, C, LANES), F32),
            pltpu.VMEM((N_HEADS, C, LANES), F32),
        ],
        compiler_params=pltpu.CompilerParams(
            dimension_semantics=("arbitrary", "arbitrary"), vmem_limit_bytes=VMEM_LIMIT),
        name="retention",
    )(proj, proj, proj)


def _merge_kernel(x_ref, a_ref, r_ref, gmix_ref, wg_ref, wdp_ref, wrp_ref, wout_ref, gffn_ref, x1_ref, h2_ref):
    x = x_ref[0]
    h = (_rms(x, NORM_EPS) * gmix_ref[...]).astype(BF16)
    swish_gate = _dot(h, wg_ref[:, :HV])
    gate_a = _dot(h, wg_ref[:, HV:HV + D_MODEL])
    gate_r = _dot(h, wg_ref[:, HV + D_MODEL:])
    r = (r_ref[0] * jax.nn.silu(swish_gate)).astype(BF16)
    merged = jax.nn.sigmoid(gate_a) * _dot(a_ref[0], wdp_ref[...]) + jax.nn.sigmoid(gate_r) * _dot(r, wrp_ref[...])
    x1 = x + _dot(merged.astype(BF16), wout_ref[...])
    x1_ref[0] = x1
    h2_ref[0] = (_rms(x1, NORM_EPS) * gffn_ref[...]).astype(BF16)


def _merge(x, a, r, g_mix, w_gates, w_diff_proj, w_ret_proj, w_out, g_ffn):
    B, S, D = x.shape
    rows = lambda n: pl.BlockSpec((1, ROW_TILE, n), lambda b, i: (b, i, 0))
    whole = lambda arr: pl.BlockSpec(arr.shape, lambda b, i: (0,) * arr.ndim)
    return pl.pallas_call(
        _merge_kernel,
        grid=(B, S // ROW_TILE),
        in_specs=[rows(D), rows(HV), rows(HV), whole(g_mix), whole(w_gates), whole(w_diff_proj),
                  whole(w_ret_proj), whole(w_out), whole(g_ffn)],
        out_specs=[rows(D), rows(D)],
        out_shape=[jax.ShapeDtypeStruct((B, S, D), F32), jax.ShapeDtypeStruct((B, S, D), BF16)],
        compiler_params=pltpu.CompilerParams(
            dimension_semantics=("arbitrary", "arbitrary"), vmem_limit_bytes=VMEM_LIMIT),
        name="merge",
    )(x, a, r, g_mix, w_gates, w_diff_proj, w_ret_proj, w_out, g_ffn)


def _ffn_kernel(x1_ref, h2_ref, wup_ref, cw_ref, cb_ref, wdown_ref, gfin_ref, o_ref, act_ref, carry_ref):
    T = ROW_TILE
    first_tile = pl.program_id(1) == 0

    @pl.when(first_tile)
    def _():
        carry_ref[...] = jnp.zeros(carry_ref.shape, F32)

    h2 = h2_ref[0]
    row = lax.broadcasted_iota(jnp.int32, (T, FF_CHUNK), 0)

    def conv(cols):
        u = _dot(h2, wup_ref[:, cols])
        prev = carry_ref[:, cols]
        prev1 = prev[SUBLANES - 1:SUBLANES, :]
        prev2 = prev[SUBLANES - 2:SUBLANES - 1, :]
        u1 = jnp.where(row == 0, prev1, pltpu.roll(u, 1, 0))
        u2 = jnp.where(row == 0, prev2, jnp.where(row == 1, prev1, pltpu.roll(u, 2, 0)))
        carry_ref[:, cols] = u[T - SUBLANES:, :]
        w = cw_ref[:, cols]
        return cb_ref[:, cols] + u * w[2:3, :] + u1 * w[1:2, :] + u2 * w[0:1, :]

    for f in range(D_FF // FF_CHUNK):
        zg = conv(slice(f * FF_CHUNK, (f + 1) * FF_CHUNK))
        zu = conv(slice(D_FF + f * FF_CHUNK, D_FF + (f + 1) * FF_CHUNK))
        act_ref[:, f * FF_CHUNK:(f + 1) * FF_CHUNK] = (jax.nn.silu(zg) * zu).astype(BF16)

    y = x1_ref[0] + _dot(act_ref[...], wdown_ref[...])
    o_ref[0] = _rms(y, NORM_EPS) * gfin_ref[...]


def _ffn(x1, h2, w_up, conv_w, conv_b, w_down, g_final):
    B, S, D = x1.shape
    rows = lambda n: pl.BlockSpec((1, ROW_TILE, n), lambda b, i: (b, i, 0))
    whole = lambda arr: pl.BlockSpec(arr.shape, lambda b, i: (0,) * arr.ndim, pipeline_mode=pl.Buffered(1))
    return pl.pallas_call(
        _ffn_kernel,
        grid=(B, S // ROW_TILE),
        in_specs=[rows(D), rows(D), whole(w_up), whole(conv_w), whole(conv_b), whole(w_down), whole(g_final)],
        out_specs=rows(D),
        out_shape=jax.ShapeDtypeStruct((B, S, D), F32),
        scratch_shapes=[pltpu.VMEM((ROW_TILE, D_FF), BF16), pltpu.VMEM((SUBLANES, 2 * D_FF), F32)],
        compiler_params=pltpu.CompilerParams(
            dimension_semantics=("arbitrary", "arbitrary"), vmem_limit_bytes=VMEM_LIMIT),
        name="ffn",
    )(x1, h2, w_up, conv_w, conv_b, w_down, g_final)


def _in_proj_weight(w_in_l):
    def per_head(off_a, off_b):
        cols = []
        for h in range(N_HEADS):
            cols.append(w_in_l[:, off_a + h * HEAD_QK: off_a + (h + 1) * HEAD_QK])
            if off_b is None:
                cols.append(jnp.zeros((D_MODEL, HEAD_QK), w_in_l.dtype))
            else:
                cols.append(w_in_l[:, off_b + h * HEAD_QK: off_b + (h + 1) * HEAD_QK])
        return cols
    cols = (per_head(_OFF_Q1, _OFF_Q2) + per_head(_OFF_K1, _OFF_K2) + per_head(_OFF_QR, None)
            + per_head(_OFF_KR, None) + [w_in_l[:, _OFF_VR:_OFF_VR + HV], w_in_l[:, _OFF_VA:_OFF_VA + HV]])
    return jnp.concatenate(cols, axis=1).astype(BF16)


def kernel(x, g_mix, w_in, lam_q1, lam_k1, lam_q2, lam_k2, g_diff_sub, w_diff_proj, w_ret_proj, w_out, g_ffn,
           w_up, conv_w, conv_b, w_down, g_final):
    depth = w_in.shape[0]
    row = lambda v: v.reshape(1, -1)
    for l in range(depth):
        lam_init = 0.8 - 0.6 * math.exp(-0.3 * l)
        proj, vt = _in_proj(x, row(g_mix[l]), _in_proj_weight(w_in[l]))
        a = _diff_attn(proj, vt, row(lam_q1[l]), row(lam_k1[l]), row(lam_q2[l]), row(lam_k2[l]),
                       row(g_diff_sub[l]), lam_init)
        r = _retention(proj)
        x1, h2 = _merge(x, a, r, row(g_mix[l]), w_in[l][:, _OFF_GATES:].astype(BF16),
                        w_diff_proj[l].astype(BF16), w_ret_proj[l].astype(BF16), w_out[l].astype(BF16),
                        row(g_ffn[l]))
        assert depth == 1
        x = _ffn(x1, h2, w_up[l].astype(BF16), conv_w[l], row(conv_b[l]), w_down[l].astype(BF16), row(g_final))
    return x
```

```python
import functools
import math

import jax
import jax.numpy as jnp
from jax import lax
from jax.experimental import pallas as pl
from jax.experimental.pallas import tpu as pltpu

D_MODEL = 1024
N_HEADS = 4
HEAD_QK = 64
HEAD_V = 128
D_FF = 2816
CONV_WIDTH = 3
NORM_EPS = 1e-6
SUBLN_EPS = 1e-5

LANES = 128
SUBLANES = 8
VMEM_LIMIT = 56 * 1024 * 1024

ROW_TILE = 512
RET_CHUNK = 256
FF_CHUNK = 256

_OFF_Q1, _OFF_Q2, _OFF_K1, _OFF_K2, _OFF_VA = 0, 256, 512, 768, 1024
_OFF_QR, _OFF_KR, _OFF_VR, _OFF_GATES = 1536, 1792, 2048, 2560
HV = N_HEADS * HEAD_V
_COL_QQ, _COL_KK, _COL_QR, _COL_KR, _COL_VR = 0, 1, 2, 3, 4
N_PROJ_BLOCKS = 5

NEG_BIG = -1e30
LOG2E = math.log2(math.e)
Q_TILE = 1024
V_ROWS = HEAD_V + 2 * SUBLANES
BF16 = jnp.bfloat16
F32 = jnp.float32


def _rms(x, eps):
    return x * lax.rsqrt(jnp.mean(x * x, axis=-1, keepdims=True) + eps)


def _dot(a, b):
    return jnp.dot(a, b, preferred_element_type=F32)


def _dot_nt(a, b):
    return lax.dot_general(a, b, (((1,), (1,)), ((), ())), preferred_element_type=F32)


def _dot_tn(a, b):
    return lax.dot_general(a, b, (((0,), (0,)), ((), ())), preferred_element_type=F32)


def _in_proj_kernel(x_ref, g_ref, w_ref, o_ref, vt_ref):
    h = (_rms(x_ref[0], NORM_EPS) * g_ref[...]).astype(BF16)
    scales = {_COL_QQ: HEAD_QK ** -0.5 * LOG2E, _COL_KR: HEAD_QK ** -0.5}
    for c in range(N_PROJ_BLOCKS):
        cols = slice(c * HV, (c + 1) * HV)
        y = _dot(h, w_ref[:, cols])
        if c in scales:
            y = y * scales[c]
        o_ref[0, :, cols] = y.astype(BF16)
    va = _dot(h, w_ref[:, N_PROJ_BLOCKS * HV:])
    vt_ref[0, 0] = va.T.astype(BF16)


def _in_proj(x, g, w):
    B, S, D = x.shape
    n_tiles = S // ROW_TILE
    n_cols = N_PROJ_BLOCKS * HV
    return pl.pallas_call(
        _in_proj_kernel,
        grid=(B, n_tiles),
        in_specs=[
            pl.BlockSpec((1, ROW_TILE, D), lambda b, i: (b, i, 0)),
            pl.BlockSpec((1, D), lambda b, i: (0, 0)),
            pl.BlockSpec((D, n_cols + HV), lambda b, i: (0, 0)),
        ],
        out_specs=[
            pl.BlockSpec((1, ROW_TILE, n_cols), lambda b, i: (b, i, 0)),
            pl.BlockSpec((1, 1, HV, ROW_TILE), lambda b, i: (b, i, 0, 0)),
        ],
        out_shape=[
            jax.ShapeDtypeStruct((B, S, n_cols), BF16),
            jax.ShapeDtypeStruct((B, n_tiles, HV, ROW_TILE), BF16),
        ],
        compiler_params=pltpu.CompilerParams(
            dimension_semantics=("arbitrary", "arbitrary"), vmem_limit_bytes=VMEM_LIMIT),
        name="in_proj",
    )(x, g, w)


def _diff_attn_kernel(lq1_ref, lk1_ref, lq2_ref, lk2_ref, gsub_ref, qq_ref, kk_ref, vt_ref, o_ref,
                      q1_ref, q2_ref, k1_ref, k2_ref, vta_ref, s1_ref, s2_ref, mx1_ref, mx2_ref,
                      acc1_ref, acc2_ref, m1_ref, m2_ref, *, lam_init):
    T = ROW_TILE
    TQ = Q_TILE
    SUBS = TQ // T
    n_tiles = k1_ref.shape[0] // T
    n_q = k1_ref.shape[0] // TQ
    h = pl.program_id(1)
    slope = jnp.where(h == 0, 2.0 ** -2, jnp.where(h == 1, 2.0 ** -4, jnp.where(h == 2, 2.0 ** -6, 2.0 ** -8)))
    slope = slope.astype(F32) * LOG2E

    lane = lax.broadcasted_iota(jnp.int32, (T, LANES), 1)
    is_qk = lane < HEAD_QK
    bias_lanes = [lane == HEAD_QK + i for i in range(3)]
    bias = lax.broadcasted_iota(jnp.int32, (T, LANES), 0).astype(F32) * slope
    hi = bias.astype(BF16).astype(F32)
    mid = (bias - hi).astype(BF16).astype(F32)
    lo = (bias - hi - mid).astype(BF16).astype(F32)
    bias_cols = jnp.where(bias_lanes[0], hi, jnp.where(bias_lanes[1], mid, jnp.where(bias_lanes[2], lo, 0.0)))
    one_cols = jnp.where(bias_lanes[0] | bias_lanes[1] | bias_lanes[2], 1.0, 0.0)
    sub_idx = lax.broadcasted_iota(jnp.int32, (2 * SUBLANES, T), 0)
    ones_rows = jnp.where(sub_idx == 0, 1.0, 0.0).astype(BF16)

    def prepare(t, carry):
        rows = pl.ds(pl.multiple_of(t * T, T), T)
        kk = kk_ref[0, rows, :].astype(F32)
        k1_ref[rows, :] = jnp.where(is_qk, kk, bias_cols).astype(BF16)
        k2_ref[rows, :] = jnp.where(is_qk, pltpu.roll(kk, HEAD_QK, 1), bias_cols).astype(BF16)
        qq = qq_ref[0, rows, :].astype(F32)
        q1_ref[rows, :] = jnp.where(is_qk, qq, one_cols).astype(BF16)
        q2_ref[rows, :] = jnp.where(is_qk, pltpu.roll(qq, HEAD_QK, 1), one_cols).astype(BF16)
        vta_ref[t, :HEAD_V, :] = vt_ref[0, t]
        vta_ref[t, HEAD_V:, :] = ones_rows
        return carry

    lax.fori_loop(0, n_tiles, prepare, 0)

    streams = ((q1_ref, k1_ref, s1_ref, mx1_ref, m1_ref, acc1_ref), (q2_ref, k2_ref, s2_ref, mx2_ref, m2_ref, acc2_ref))
    full = slice(0, TQ)

    def scores(qi, j, slot, sub, cols=full, masked=False):
        q_rows = pl.ds(pl.multiple_of(qi * TQ, TQ) + cols.start, cols.stop - cols.start)
        k_rows = pl.ds(pl.multiple_of(j * T, T), T)
        for q_ref, k_ref, s_ref, mx_ref, _, _ in streams:
            s = _dot_nt(k_ref[k_rows, :], q_ref[q_rows, :])
            if masked:
                key_idx = lax.broadcasted_iota(jnp.int32, s.shape, 0)
                qry_idx = lax.broadcasted_iota(jnp.int32, s.shape, 1)
                s = jnp.where(key_idx <= qry_idx, s, NEG_BIG)
            s_ref[slot, sub, :, cols] = s
            mx_ref[slot, sub, :, cols] = jnp.max(s, axis=0, keepdims=True)

    def accumulate(qi, items, cols=full):
        for _, _, s_ref, mx_ref, m_ref, acc_ref in streams:
            offsets = [slope * jnp.asarray(j * T - qi * TQ, F32) for j, _, _ in items]
            m_old = m_ref[:, cols]
            m_new = m_old
            for (j, slot, sub), offset in zip(items, offsets):
                m_new = jnp.maximum(m_new, mx_ref[slot, sub, :, cols] + offset)
            alpha = jnp.exp2(m_old - m_new)
            pv = None
            for (j, slot, sub), offset in zip(items, offsets):
                p = jnp.exp2((s_ref[slot, sub, :, cols] - (m_new - offset)).astype(BF16))
                d = _dot(vta_ref[j], p)
                pv = d if pv is None else pv + d
            acc_ref[:, cols] = alpha * acc_ref[:, cols] + pv
            m_ref[:, cols] = m_new

    def reset():
        for _, _, _, _, m_ref, acc_ref in streams:
            m_ref[...] = jnp.full(m_ref.shape, NEG_BIG, F32)
            acc_ref[...] = jnp.zeros(acc_ref.shape, F32)

    lam = (jnp.exp(jnp.sum(lq1_ref[...] * lk1_ref[...], axis=-1, keepdims=True))
           - jnp.exp(jnp.sum(lq2_ref[...] * lk2_ref[...], axis=-1, keepdims=True)) + lam_init)
    out_gain = gsub_ref[...] * (1.0 - lam_init)

    def finish(qi):
        o1 = acc1_ref[:HEAD_V, :] * (1.0 / acc1_ref[HEAD_V:HEAD_V + 1, :])
        o2 = acc2_ref[:HEAD_V, :] * (1.0 / acc2_ref[HEAD_V:HEAD_V + 1, :])
        a = o1 - lam * o2
        a = a * lax.rsqrt(jnp.mean(a * a, axis=0, keepdims=True) + SUBLN_EPS)
        o_ref[0, pl.ds(pl.multiple_of(qi * TQ, TQ), TQ), :] = (a.T * out_gain).astype(BF16)

    def scores_full(qi, step, slot):
        for sub in range(SUBS):
            scores(qi, step * SUBS + sub, slot, sub)

    def accumulate_full(qi, step, slot):
        accumulate(qi, [(step * SUBS + sub, slot, sub) for sub in range(SUBS)])

    def scores_diagonal(qi, slot):
        for sub in range(SUBS):
            scores(qi, qi * SUBS + sub, slot, sub, cols=slice(sub * T, TQ), masked=True)

    def accumulate_diagonal(qi, slot):
        for sub in range(SUBS):
            accumulate(qi, [(qi * SUBS + sub, slot, sub)], cols=slice(sub * T, TQ))

    reset()
    scores_diagonal(0, 0)
    accumulate_diagonal(0, 0)
    scores_full(min(1, n_q - 1), 0, 1)
    finish(0)

    def q_tile(qi, slot0):
        reset()

        def step(t, carry):
            accumulate_full(qi, t, (slot0 + t) % 2)
            scores_full(qi, t + 1, (slot0 + t + 1) % 2)
            return carry

        lax.fori_loop(0, qi - 1, step, 0)
        accumulate_full(qi, qi - 1, (slot0 + qi - 1) % 2)
        scores_diagonal(qi, (slot0 + qi) % 2)
        accumulate_diagonal(qi, (slot0 + qi) % 2)
        next_slot0 = (slot0 + qi + 1) % 2
        scores_full(jnp.minimum(qi + 1, n_q - 1), 0, next_slot0)
        finish(qi)
        return next_slot0

    lax.fori_loop(1, n_q, q_tile, jnp.int32(1))


def _diff_attn(proj, vt, lam_q1, lam_k1, lam_q2, lam_k2, g_sub, lam_init):
    B, S, _ = proj.shape
    T = ROW_TILE
    n_tiles = S // T
    TQ = Q_TILE
    subs = TQ // T
    assert S % TQ == 0 and TQ % T == 0
    small = lambda n: pl.BlockSpec((1, n), lambda b, h: (0, 0))
    seq = lambda: pltpu.VMEM((S, LANES), BF16)
    return pl.pallas_call(
        functools.partial(_diff_attn_kernel, lam_init=lam_init),
        grid=(B, N_HEADS),
        in_specs=[
            small(HEAD_QK), small(HEAD_QK), small(HEAD_QK), small(HEAD_QK), small(HEAD_V),
            pl.BlockSpec((1, S, LANES), lambda b, h: (b, 0, _COL_QQ * N_HEADS + h)),
            pl.BlockSpec((1, S, LANES), lambda b, h: (b, 0, _COL_KK * N_HEADS + h)),
            pl.BlockSpec((1, n_tiles, HEAD_V, T), lambda b, h: (b, 0, h, 0)),
        ],
        out_specs=pl.BlockSpec((1, S, HEAD_V), lambda b, h: (b, 0, h)),
        out_shape=jax.ShapeDtypeStruct((B, S, HV), BF16),
        scratch_shapes=[
            seq(), seq(), seq(), seq(),
            pltpu.VMEM((n_tiles, V_ROWS, T), BF16),
            pltpu.VMEM((2, subs, T, TQ), F32), pltpu.VMEM((2, subs, T, TQ), F32),
            pltpu.VMEM((2, subs, 1, TQ), F32), pltpu.VMEM((2, subs, 1, TQ), F32),
            pltpu.VMEM((V_ROWS, TQ), F32), pltpu.VMEM((V_ROWS, TQ), F32),
            pltpu.VMEM((1, TQ), F32), pltpu.VMEM((1, TQ), F32),
        ],
        compiler_params=pltpu.CompilerParams(
            dimension_semantics=("arbitrary", "arbitrary"), vmem_limit_bytes=VMEM_LIMIT),
        name="diff_attn",
    )(lam_q1, lam_k1, lam_q2, lam_k2, g_sub, proj, proj, vt)


def _retention_kernel(q_ref, k_ref, v_ref, o_ref, state_ref, decay_ref, qdec_ref, kdec_ref):
    C = RET_CHUNK
    log_g = [math.log(1.0 - 2.0 ** (-5.0 - h)) for h in range(N_HEADS)]

    @pl.when((pl.program_id(0) == 0) & (pl.program_id(1) == 0))
    def _():
        row = lax.broadcasted_iota(jnp.int32, (C, C), 0)
        col = lax.broadcasted_iota(jnp.int32, (C, C), 1)
        rel = (row - col).astype(F32)
        pos = lax.broadcasted_iota(jnp.int32, (C, LANES), 0).astype(F32)
        for h in range(N_HEADS):
            decay_ref[h] = jnp.where(rel >= 0, jnp.exp(log_g[h] * jnp.maximum(rel, 0.0)), 0.0)
            qdec_ref[h] = jnp.exp(log_g[h] * (pos + 1.0))
            kdec_ref[h] = jnp.exp(log_g[h] * (C - 1.0 - pos))

    @pl.when(pl.program_id(1) == 0)
    def _():
        state_ref[...] = jnp.zeros(state_ref.shape, F32)

    for h in range(N_HEADS):
        cols = slice(h * LANES, (h + 1) * LANES)
        q = q_ref[0, :, cols]
        k = k_ref[0, :, cols]
        v = v_ref[0, :, cols]
        scores = _dot_nt(q, k) * decay_ref[h]
        state = state_ref[h]
        o = _dot(scores.astype(BF16), v) + _dot((q.astype(F32) * qdec_ref[h]).astype(BF16), state.astype(BF16))
        state_ref[h] = state * math.exp(log_g[h] * C) + _dot_tn((k.astype(F32) * kdec_ref[h]).astype(BF16), v)
        o_ref[0, :, cols] = _rms(o, SUBLN_EPS)


def _retention(proj):
    B, S, _ = proj.shape
    C = RET_CHUNK
    col = lambda c: pl.BlockSpec((1, C, HV), lambda b, i: (b, i, c))
    return pl.pallas_call(
        _retention_kernel,
        grid=(B, S // C),
        in_specs=[col(_COL_QR), col(_COL_KR), col(_COL_VR)],
        out_specs=pl.BlockSpec((1, C, HV), lambda b, i: (b, i, 0)),
        out_shape=jax.ShapeDtypeStruct((B, S, HV), F32),
        scratch_shapes=[
            pltpu.VMEM((N_HEADS, LANES, HEAD_V), F32),
            pltpu.VMEM((N_HEADS, C, C), F32),
            pltpu.VMEM((N_HEADS, C, LANES), F32),
            pltpu.VMEM((N_HEADS, C, LANES), F32),
        ],
        compiler_params=pltpu.CompilerParams(
            dimension_semantics=("arbitrary", "arbitrary"), vmem_limit_bytes=VMEM_LIMIT),
        name="retention",
    )(proj, proj, proj)


def _merge_kernel(x_ref, a_ref, r_ref, gmix_ref, wg_ref, wdp_ref, wrp_ref, wout_ref, gffn_ref, x1_ref, h2_ref):
    x = x_ref[0]
    h = (_rms(x, NORM_EPS) * gmix_ref[...]).astype(BF16)
    swish_gate = _dot(h, wg_ref[:, :HV])
    gate_a = _dot(h, wg_ref[:, HV:HV + D_MODEL])
    gate_r = _dot(h, wg_ref[:, HV + D_MODEL:])
    r = (r_ref[0] * jax.nn.silu(swish_gate)).astype(BF16)
    merged = jax.nn.sigmoid(gate_a) * _dot(a_ref[0], wdp_ref[...]) + jax.nn.sigmoid(gate_r) * _dot(r, wrp_ref[...])
    x1 = x + _dot(merged.astype(BF16), wout_ref[...])
    x1_ref[0] = x1
    h2_ref[0] = (_rms(x1, NORM_EPS) * gffn_ref[...]).astype(BF16)


def _merge(x, a, r, g_mix, w_gates, w_diff_proj, w_ret_proj, w_out, g_ffn):
    B, S, D = x.shape
    rows = lambda n: pl.BlockSpec((1, ROW_TILE, n), lambda b, i: (b, i, 0))
    whole = lambda arr: pl.BlockSpec(arr.shape, lambda b, i: (0,) * arr.ndim)
    return pl.pallas_call(
        _merge_kernel,
        grid=(B, S // ROW_TILE),
        in_specs=[rows(D), rows(HV), rows(HV), whole(g_mix), whole(w_gates), whole(w_diff_proj),
                  whole(w_ret_proj), whole(w_out), whole(g_ffn)],
        out_specs=[rows(D), rows(D)],
        out_shape=[jax.ShapeDtypeStruct((B, S, D), F32), jax.ShapeDtypeStruct((B, S, D), BF16)],
        compiler_params=pltpu.CompilerParams(
            dimension_semantics=("arbitrary", "arbitrary"), vmem_limit_bytes=VMEM_LIMIT),
        name="merge",
    )(x, a, r, g_mix, w_gates, w_diff_proj, w_ret_proj, w_out, g_ffn)


def _ffn_kernel(x1_ref, h2_ref, wup_ref, cw_ref, cb_ref, wdown_ref, gfin_ref, o_ref, act_ref, carry_ref):
    T = ROW_TILE
    first_tile = pl.program_id(1) == 0

    @pl.when(first_tile)
    def _():
        carry_ref[...] = jnp.zeros(carry_ref.shape, F32)

    h2 = h2_ref[0]
    row = lax.broadcasted_iota(jnp.int32, (T, FF_CHUNK), 0)

    def conv(cols):
        u = _dot(h2, wup_ref[:, cols])
        prev = carry_ref[:, cols]
        prev1 = prev[SUBLANES - 1:SUBLANES, :]
        prev2 = prev[SUBLANES - 2:SUBLANES - 1, :]
        u1 = jnp.where(row == 0, prev1, pltpu.roll(u, 1, 0))
        u2 = jnp.where(row == 0, prev2, jnp.where(row == 1, prev1, pltpu.roll(u, 2, 0)))
        carry_ref[:, cols] = u[T - SUBLANES:, :]
        w = cw_ref[:, cols]
        return cb_ref[:, cols] + u * w[2:3, :] + u1 * w[1:2, :] + u2 * w[0:1, :]

    for f in range(D_FF // FF_CHUNK):
        zg = conv(slice(f * FF_CHUNK, (f + 1) * FF_CHUNK))
        zu = conv(slice(D_FF + f * FF_CHUNK, D_FF + (f + 1) * FF_CHUNK))
        act_ref[:, f * FF_CHUNK:(f + 1) * FF_CHUNK] = (jax.nn.silu(zg) * zu).astype(BF16)

    y = x1_ref[0] + _dot(act_ref[...], wdown_ref[...])
    o_ref[0] = _rms(y, NORM_EPS) * gfin_ref[...]


def _ffn(x1, h2, w_up, conv_w, conv_b, w_down, g_final):
    B, S, D = x1.shape
    rows = lambda n: pl.BlockSpec((1, ROW_TILE, n), lambda b, i: (b, i, 0))
    whole = lambda arr: pl.BlockSpec(arr.shape, lambda b, i: (0,) * arr.ndim, pipeline_mode=pl.Buffered(1))
    return pl.pallas_call(
        _ffn_kernel,
        grid=(B, S // ROW_TILE),
        in_specs=[rows(D), rows(D), whole(w_up), whole(conv_w), whole(conv_b), whole(w_down), whole(g_final)],
        out_specs=rows(D),
        out_shape=jax.ShapeDtypeStruct((B, S, D), F32),
        scratch_shapes=[pltpu.VMEM((ROW_TILE, D_FF), BF16), pltpu.VMEM((SUBLANES, 2 * D_FF), F32)],
        compiler_params=pltpu.CompilerParams(
            dimension_semantics=("arbitrary", "arbitrary"), vmem_limit_bytes=VMEM_LIMIT),
        name="ffn",
    )(x1, h2, w_up, conv_w, conv_b, w_down, g_final)


def _in_proj_weight(w_in_l):
    def per_head(off_a, off_b):
        cols = []
        for h in range(N_HEADS):
            cols.append(w_in_l[:, off_a + h * HEAD_QK: off_a + (h + 1) * HEAD_QK])
            if off_b is None:
                cols.append(jnp.zeros((D_MODEL, HEAD_QK), w_in_l.dtype))
            else:
                cols.append(w_in_l[:, off_b + h * HEAD_QK: off_b + (h + 1) * HEAD_QK])
        return cols
    cols = (per_head(_OFF_Q1, _OFF_Q2) + per_head(_OFF_K1, _OFF_K2) + per_head(_OFF_QR, None)
            + per_head(_OFF_KR, None) + [w_in_l[:, _OFF_VR:_OFF_VR + HV], w_in_l[:, _OFF_VA:_OFF_VA + HV]])
    return jnp.concatenate(cols, axis=1).astype(BF16)


def kernel(x, g_mix, w_in, lam_q1, lam_k1, lam_q2, lam_k2, g_diff_sub, w_diff_proj, w_ret_proj, w_out, g_ffn,
           w_up, conv_w, conv_b, w_down, g_final):
    depth = w_in.shape[0]
    row = lambda v: v.reshape(1, -1)
    for l in range(depth):
        lam_init = 0.8 - 0.6 * math.exp(-0.3 * l)
        proj, vt = _in_proj(x, row(g_mix[l]), _in_proj_weight(w_in[l]))
        a = _diff_attn(proj, vt, row(lam_q1[l]), row(lam_k1[l]), row(lam_q2[l]), row(lam_k2[l]),
                       row(g_diff_sub[l]), lam_init)
        r = _retention(proj)
        x1, h2 = _merge(x, a, r, row(g_mix[l]), w_in[l][:, _OFF_GATES:].astype(BF16),
                        w_diff_proj[l].astype(BF16), w_ret_proj[l].astype(BF16), w_out[l].astype(BF16),
                        row(g_ffn[l]))
        assert depth == 1
        x = _ffn(x1, h2, w_up[l].astype(BF16), conv_w[l], row(conv_b[l]), w_down[l].astype(BF16), row(g_final))
    return x
```

```python
import functools
import math

import jax
import jax.numpy as jnp
from jax import lax
from jax.experimental import pallas as pl
from jax.experimental.pallas import tpu as pltpu

D_MODEL = 1024
N_HEADS = 4
HEAD_QK = 64
HEAD_V = 128
D_FF = 2816
CONV_WIDTH = 3
NORM_EPS = 1e-6
SUBLN_EPS = 1e-5

LANES = 128
SUBLANES = 8
VMEM_LIMIT = 56 * 1024 * 1024

ROW_TILE = 512
RET_CHUNK = 256
FF_CHUNK = 256

_OFF_Q1, _OFF_Q2, _OFF_K1, _OFF_K2, _OFF_VA = 0, 256, 512, 768, 1024
_OFF_QR, _OFF_KR, _OFF_VR, _OFF_GATES = 1536, 1792, 2048, 2560
HV = N_HEADS * HEAD_V
_COL_QQ, _COL_KK, _COL_QKR, _COL_VR = 0, 1, 2, 3
N_PROJ_BLOCKS = 4

NEG_BIG = -1e30
LOG2E = math.log2(math.e)
Q_TILE = 1024
V_ROWS = HEAD_V + 2 * SUBLANES
BF16 = jnp.bfloat16
F32 = jnp.float32


def _rms(x, eps):
    return x * lax.rsqrt(jnp.mean(x * x, axis=-1, keepdims=True) + eps)


def _dot(a, b):
    return jnp.dot(a, b, preferred_element_type=F32)


def _dot_nt(a, b):
    return lax.dot_general(a, b, (((1,), (1,)), ((), ())), preferred_element_type=F32)


def _dot_tn(a, b):
    return lax.dot_general(a, b, (((0,), (0,)), ((), ())), preferred_element_type=F32)


def _in_proj_kernel(x_ref, g_ref, w_ref, o_ref, vt_ref):
    h = (_rms(x_ref[0], NORM_EPS) * g_ref[...]).astype(BF16)
    scales = {_COL_QQ: HEAD_QK ** -0.5 * LOG2E}
    for c in range(N_PROJ_BLOCKS):
        cols = slice(c * HV, (c + 1) * HV)
        y = _dot(h, w_ref[:, cols])
        if c in scales:
            y = y * scales[c]
        o_ref[0, :, cols] = y.astype(BF16)
    va = _dot(h, w_ref[:, N_PROJ_BLOCKS * HV:])
    vt_ref[0, 0] = va.T.astype(BF16)


def _in_proj(x, g, w):
    B, S, D = x.shape
    n_tiles = S // ROW_TILE
    n_cols = N_PROJ_BLOCKS * HV
    return pl.pallas_call(
        _in_proj_kernel,
        grid=(B, n_tiles),
        in_specs=[
            pl.BlockSpec((1, ROW_TILE, D), lambda b, i: (b, i, 0)),
            pl.BlockSpec((1, D), lambda b, i: (0, 0)),
            pl.BlockSpec((D, n_cols + HV), lambda b, i: (0, 0)),
        ],
        out_specs=[
            pl.BlockSpec((1, ROW_TILE, n_cols), lambda b, i: (b, i, 0)),
            pl.BlockSpec((1, 1, HV, ROW_TILE), lambda b, i: (b, i, 0, 0)),
        ],
        out_shape=[
            jax.ShapeDtypeStruct((B, S, n_cols), BF16),
            jax.ShapeDtypeStruct((B, n_tiles, HV, ROW_TILE), BF16),
        ],
        compiler_params=pltpu.CompilerParams(
            dimension_semantics=("arbitrary", "arbitrary"), vmem_limit_bytes=VMEM_LIMIT),
        name="in_proj",
    )(x, g, w)


def _diff_attn_kernel(lq1_ref, lk1_ref, lq2_ref, lk2_ref, gsub_ref, qq_ref, kk_ref, vt_ref, o_ref,
                      q1_ref, q2_ref, k1_ref, k2_ref, vta_ref, s1a_ref, s1b_ref, s2a_ref, s2b_ref,
                      mx1a_ref, mx1b_ref, mx2a_ref, mx2b_ref, acc1_ref, acc2_ref, m1_ref, m2_ref, *, lam_init):
    T = ROW_TILE
    TQ = Q_TILE
    SUBS = TQ // T
    n_tiles = k1_ref.shape[0] // T
    n_q = k1_ref.shape[0] // TQ
    h = pl.program_id(1)
    slope = jnp.where(h == 0, 2.0 ** -2, jnp.where(h == 1, 2.0 ** -4, jnp.where(h == 2, 2.0 ** -6, 2.0 ** -8)))
    slope = slope.astype(F32) * LOG2E

    lane = lax.broadcasted_iota(jnp.int32, (T, LANES), 1)
    is_qk = lane < HEAD_QK
    bias_lanes = [lane == HEAD_QK + i for i in range(3)]
    bias = lax.broadcasted_iota(jnp.int32, (T, LANES), 0).astype(F32) * slope
    hi = bias.astype(BF16).astype(F32)
    mid = (bias - hi).astype(BF16).astype(F32)
    lo = (bias - hi - mid).astype(BF16).astype(F32)
    bias_cols = jnp.where(bias_lanes[0], hi, jnp.where(bias_lanes[1], mid, jnp.where(bias_lanes[2], lo, 0.0)))
    one_cols = jnp.where(bias_lanes[0] | bias_lanes[1] | bias_lanes[2], 1.0, 0.0)
    sub_idx = lax.broadcasted_iota(jnp.int32, (2 * SUBLANES, T), 0)
    ones_rows = jnp.where(sub_idx == 0, 1.0, 0.0).astype(BF16)

    def prepare(t, carry):
        rows = pl.ds(pl.multiple_of(t * T, T), T)
        kk = kk_ref[0, rows, :].astype(F32)
        k1_ref[rows, :] = jnp.where(is_qk, kk, bias_cols).astype(BF16)
        k2_ref[rows, :] = jnp.where(is_qk, pltpu.roll(kk, HEAD_QK, 1), bias_cols).astype(BF16)
        qq = qq_ref[0, rows, :].astype(F32)
        q1_ref[rows, :] = jnp.where(is_qk, qq, one_cols).astype(BF16)
        q2_ref[rows, :] = jnp.where(is_qk, pltpu.roll(qq, HEAD_QK, 1), one_cols).astype(BF16)
        vta_ref[t, :HEAD_V, :] = vt_ref[0, t]
        vta_ref[t, HEAD_V:, :] = ones_rows
        return carry

    lax.fori_loop(0, n_tiles, prepare, 0)

    streams = ((q1_ref, k1_ref, (s1a_ref, s1b_ref), (mx1a_ref, mx1b_ref), m1_ref, acc1_ref),
               (q2_ref, k2_ref, (s2a_ref, s2b_ref), (mx2a_ref, mx2b_ref), m2_ref, acc2_ref))
    full = slice(0, TQ)

    def scores(stream, qi, j, slot, sub, cols=full, masked=False):
        q_ref, k_ref, s_ref, mx_ref, _, _ = stream
        q_rows = pl.ds(pl.multiple_of(qi * TQ, TQ) + cols.start, cols.stop - cols.start)
        k_rows = pl.ds(pl.multiple_of(j * T, T), T)
        s = _dot_nt(k_ref[k_rows, :], q_ref[q_rows, :])
        if masked:
            key_idx = lax.broadcasted_iota(jnp.int32, s.shape, 0)
            qry_idx = lax.broadcasted_iota(jnp.int32, s.shape, 1)
            s = jnp.where(key_idx <= qry_idx, s, NEG_BIG)
        s_ref[slot][sub, :, cols] = s
        mx_ref[slot][sub, :, cols] = jnp.max(s, axis=0, keepdims=True)

    def update(stream, qi, items, cols, score_jobs):
        _, _, s_ref, mx_ref, m_ref, acc_ref = stream
        offsets = [slope * jnp.asarray(j * T - qi * TQ, F32) for j, _, _ in items]
        m_old = m_ref[:, cols]
        m_new = m_old
        for (j, slot, sub), offset in zip(items, offsets):
            m_new = jnp.maximum(m_new, mx_ref[slot][sub, :, cols] + offset)
        alpha = jnp.exp2(m_old - m_new)
        m_ref[:, cols] = m_new
        pv = None
        for (j, slot, sub), offset in zip(items, offsets):
            if score_jobs:
                score_jobs.pop(0)()
            p = jnp.exp2((s_ref[slot][sub, :, cols] - (m_new - offset)).astype(BF16))
            d = _dot(vta_ref[j], p)
            pv = d if pv is None else pv + d
        acc_ref[:, cols] = alpha * acc_ref[:, cols] + pv

    def reset():
        for _, _, _, _, m_ref, acc_ref in streams:
            m_ref[...] = jnp.full(m_ref.shape, NEG_BIG, F32)
            acc_ref[...] = jnp.zeros(acc_ref.shape, F32)

    lam = (jnp.exp(jnp.sum(lq1_ref[...] * lk1_ref[...], axis=-1, keepdims=True))
           - jnp.exp(jnp.sum(lq2_ref[...] * lk2_ref[...], axis=-1, keepdims=True)) + lam_init)
    out_gain = gsub_ref[...] * (1.0 - lam_init)

    def finish(qi):
        o1 = acc1_ref[:HEAD_V, :] * (1.0 / acc1_ref[HEAD_V:HEAD_V + 1, :])
        o2 = acc2_ref[:HEAD_V, :] * (1.0 / acc2_ref[HEAD_V:HEAD_V + 1, :])
        a = o1 - lam * o2
        a = a * lax.rsqrt(jnp.mean(a * a, axis=0, keepdims=True) + SUBLN_EPS)
        o_ref[0, pl.ds(pl.multiple_of(qi * TQ, TQ), TQ), :] = (a.T * out_gain).astype(BF16)

    def scores_full(qi, step, slot):
        return lambda stream: [functools.partial(scores, stream, qi, step * SUBS + sub, slot, sub) for sub in range(SUBS)]

    def full_items(step, slot):
        return [(step * SUBS + sub, slot, sub) for sub in range(SUBS)]

    def scores_diagonal(qi, slot):
        return lambda stream: [functools.partial(scores, stream, qi, qi * SUBS + sub, slot, sub,
                                                 cols=slice(sub * T, TQ), masked=True) for sub in range(SUBS)]

    def diagonal_updates(qi, slot):
        return [([(qi * SUBS + sub, slot, sub) for sub in range(c + 1)], slice(c * T, (c + 1) * T)) for c in range(SUBS)]

    def stage(qi, updates, score_jobs_of):
        for stream in streams:
            jobs = score_jobs_of(stream)
            for items, cols in updates:
                update(stream, qi, items, cols, jobs)
            for job in jobs:
                job()

    reset()
    for stream in streams:
        for job in scores_diagonal(0, 0)(stream):
            job()
    stage(0, diagonal_updates(0, 0), scores_full(min(1, n_q - 1), 0, 1))
    finish(0)

    def by_parity(value, fn):
        for parity in range(2):
            pl.when(value % 2 == parity)(functools.partial(fn, parity))

    def q_tile(qi, slot0):
        reset()

        def step(t, carry):
            def run(cur):
                stage(qi, [(full_items(t, cur), full)], scores_full(qi, t + 1, 1 - cur))
            by_parity(slot0 + t, run)
            return carry

        lax.fori_loop(0, qi - 1, step, 0)

        def tail(cur):
            stage(qi, [(full_items(qi - 1, cur), full)], scores_diagonal(qi, 1 - cur))
            stage(qi, diagonal_updates(qi, 1 - cur), scores_full(jnp.minimum(qi + 1, n_q - 1), 0, cur))
            finish(qi)
        by_parity(slot0 + qi - 1, tail)
        return (slot0 + qi + 1) % 2

    lax.fori_loop(1, n_q, q_tile, jnp.int32(1))


def _diff_attn(proj, vt, lam_q1, lam_k1, lam_q2, lam_k2, g_sub, lam_init):
    B, S, _ = proj.shape
    T = ROW_TILE
    n_tiles = S // T
    TQ = Q_TILE
    subs = TQ // T
    assert S % TQ == 0 and TQ % T == 0
    small = lambda n: pl.BlockSpec((1, n), lambda b, h: (0, 0))
    seq = lambda: pltpu.VMEM((S, LANES), BF16)
    return pl.pallas_call(
        functools.partial(_diff_attn_kernel, lam_init=lam_init),
        grid=(B, N_HEADS),
        in_specs=[
            small(HEAD_QK), small(HEAD_QK), small(HEAD_QK), small(HEAD_QK), small(HEAD_V),
            pl.BlockSpec((1, S, LANES), lambda b, h: (b, 0, _COL_QQ * N_HEADS + h)),
            pl.BlockSpec((1, S, LANES), lambda b, h: (b, 0, _COL_KK * N_HEADS + h)),
            pl.BlockSpec((1, n_tiles, HEAD_V, T), lambda b, h: (b, 0, h, 0)),
        ],
        out_specs=pl.BlockSpec((1, S, HEAD_V), lambda b, h: (b, 0, h)),
        out_shape=jax.ShapeDtypeStruct((B, S, HV), BF16),
        scratch_shapes=[
            seq(), seq(), seq(), seq(),
            pltpu.VMEM((n_tiles, V_ROWS, T), BF16),
            *[pltpu.VMEM((subs, T, TQ), F32)] * 4,
            *[pltpu.VMEM((subs, 1, TQ), F32)] * 4,
            pltpu.VMEM((V_ROWS, TQ), F32), pltpu.VMEM((V_ROWS, TQ), F32),
            pltpu.VMEM((1, TQ), F32), pltpu.VMEM((1, TQ), F32),
        ],
        compiler_params=pltpu.CompilerParams(
            dimension_semantics=("arbitrary", "arbitrary"), vmem_limit_bytes=VMEM_LIMIT),
        name="diff_attn",
    )(lam_q1, lam_k1, lam_q2, lam_k2, g_sub, proj, proj, vt)


def _retention_kernel(qk_ref, v_ref, o_ref, state_ref, decay_ref, qdec_ref, kdec_ref):
    C = RET_CHUNK
    log_g = [math.log(1.0 - 2.0 ** (-5.0 - h)) for h in range(N_HEADS)]

    @pl.when((pl.program_id(0) == 0) & (pl.program_id(1) == 0))
    def _():
        row = lax.broadcasted_iota(jnp.int32, (C, C), 0)
        col = lax.broadcasted_iota(jnp.int32, (C, C), 1)
        rel = (row - col).astype(F32)
        pos = lax.broadcasted_iota(jnp.int32, (C, LANES), 0).astype(F32)
        for h in range(N_HEADS):
            decay_ref[h] = jnp.where(rel >= 0, jnp.exp(log_g[h] * jnp.maximum(rel, 0.0)), 0.0)
            qdec_ref[h] = jnp.exp(log_g[h] * (pos + 1.0))
            kdec_ref[h] = jnp.exp(log_g[h] * (C - 1.0 - pos))

    @pl.when(pl.program_id(1) == 0)
    def _():
        state_ref[...] = jnp.zeros(state_ref.shape, F32)

    is_q = lax.broadcasted_iota(jnp.int32, (C, LANES), 1) < HEAD_QK
    for h in range(N_HEADS):
        cols = slice(h * LANES, (h + 1) * LANES)
        qk = qk_ref[0, :, cols].astype(F32)
        q = jnp.where(is_q, qk, 0.0)
        k = jnp.where(is_q, pltpu.roll(qk, HEAD_QK, 1), 0.0) * HEAD_QK ** -0.5
        v = v_ref[0, :, cols]
        scores = _dot_nt(q.astype(BF16), k.astype(BF16)) * decay_ref[h]
        state = state_ref[h]
        o = _dot(scores.astype(BF16), v) + _dot((q * qdec_ref[h]).astype(BF16), state.astype(BF16))
        state_ref[h] = state * math.exp(log_g[h] * C) + _dot_tn((k * kdec_ref[h]).astype(BF16), v)
        o_ref[0, :, cols] = _rms(o, SUBLN_EPS)


def _retention(proj):
    B, S, _ = proj.shape
    C = RET_CHUNK
    col = lambda c: pl.BlockSpec((1, C, HV), lambda b, i: (b, i, c))
    return pl.pallas_call(
        _retention_kernel,
        grid=(B, S // C),
        in_specs=[col(_COL_QKR), col(_COL_VR)],
        out_specs=pl.BlockSpec((1, C, HV), lambda b, i: (b, i, 0)),
        out_shape=jax.ShapeDtypeStruct((B, S, HV), F32),
        scratch_shapes=[
            pltpu.VMEM((N_HEADS, LANES, HEAD_V), F32),
            pltpu.VMEM((N_HEADS, C, C), F32),
            pltpu.VMEM((N_HEADS, C, LANES), F32),
            pltpu.VMEM((N_HEADS, C, LANES), F32),
        ],
        compiler_params=pltpu.CompilerParams(
            dimension_semantics=("arbitrary", "arbitrary"), vmem_limit_bytes=VMEM_LIMIT),
        name="retention",
    )(proj, proj)


def _merge_kernel(x_ref, a_ref, r_ref, gmix_ref, wg_ref, wdp_ref, wrp_ref, wout_ref, gffn_ref, x1_ref, h2_ref):
    x = x_ref[0]
    h = (_rms(x, NORM_EPS) * gmix_ref[...]).astype(BF16)
    swish_gate = _dot(h, wg_ref[:, :HV])
    gate_a = _dot(h, wg_ref[:, HV:HV + D_MODEL])
    gate_r = _dot(h, wg_ref[:, HV + D_MODEL:])
    r = (r_ref[0] * jax.nn.silu(swish_gate)).astype(BF16)
    merged = jax.nn.sigmoid(gate_a) * _dot(a_ref[0], wdp_ref[...]) + jax.nn.sigmoid(gate_r) * _dot(r, wrp_ref[...])
    x1 = x + _dot(merged.astype(BF16), wout_ref[...])
    x1_ref[0] = x1
    h2_ref[0] = (_rms(x1, NORM_EPS) * gffn_ref[...]).astype(BF16)


def _merge(x, a, r, g_mix, w_gates, w_diff_proj, w_ret_proj, w_out, g_ffn):
    B, S, D = x.shape
    rows = lambda n: pl.BlockSpec((1, ROW_TILE, n), lambda b, i: (b, i, 0))
    whole = lambda arr: pl.BlockSpec(arr.shape, lambda b, i: (0,) * arr.ndim)
    return pl.pallas_call(
        _merge_kernel,
        grid=(B, S // ROW_TILE),
        in_specs=[rows(D), rows(HV), rows(HV), whole(g_mix), whole(w_gates), whole(w_diff_proj),
                  whole(w_ret_proj), whole(w_out), whole(g_ffn)],
        out_specs=[rows(D), rows(D)],
        out_shape=[jax.ShapeDtypeStruct((B, S, D), F32), jax.ShapeDtypeStruct((B, S, D), BF16)],
        compiler_params=pltpu.CompilerParams(
            dimension_semantics=("arbitrary", "arbitrary"), vmem_limit_bytes=VMEM_LIMIT),
        name="merge",
    )(x, a, r, g_mix, w_gates, w_diff_proj, w_ret_proj, w_out, g_ffn)


def _ffn_kernel(x1_ref, h2_ref, wup_ref, cw_ref, cb_ref, wdown_ref, gfin_ref, o_ref, act_ref, carry_ref):
    T = ROW_TILE
    first_tile = pl.program_id(1) == 0

    @pl.when(first_tile)
    def _():
        carry_ref[...] = jnp.zeros(carry_ref.shape, F32)

    h2 = h2_ref[0]
    row = lax.broadcasted_iota(jnp.int32, (T, FF_CHUNK), 0)

    def conv(cols):
        u = _dot(h2, wup_ref[:, cols])
        prev = carry_ref[:, cols]
        prev1 = prev[SUBLANES - 1:SUBLANES, :]
        prev2 = prev[SUBLANES - 2:SUBLANES - 1, :]
        u1 = jnp.where(row == 0, prev1, pltpu.roll(u, 1, 0))
        u2 = jnp.where(row == 0, prev2, jnp.where(row == 1, prev1, pltpu.roll(u, 2, 0)))
        carry_ref[:, cols] = u[T - SUBLANES:, :]
        w = cw_ref[:, cols]
        return cb_ref[:, cols] + u * w[2:3, :] + u1 * w[1:2, :] + u2 * w[0:1, :]

    for f in range(D_FF // FF_CHUNK):
        zg = conv(slice(f * FF_CHUNK, (f + 1) * FF_CHUNK))
        zu = conv(slice(D_FF + f * FF_CHUNK, D_FF + (f + 1) * FF_CHUNK))
        act_ref[:, f * FF_CHUNK:(f + 1) * FF_CHUNK] = (jax.nn.silu(zg) * zu).astype(BF16)

    y = x1_ref[0] + _dot(act_ref[...], wdown_ref[...])
    o_ref[0] = _rms(y, NORM_EPS) * gfin_ref[...]


def _ffn(x1, h2, w_up, conv_w, conv_b, w_down, g_final):
    B, S, D = x1.shape
    rows = lambda n: pl.BlockSpec((1, ROW_TILE, n), lambda b, i: (b, i, 0))
    whole = lambda arr: pl.BlockSpec(arr.shape, lambda b, i: (0,) * arr.ndim, pipeline_mode=pl.Buffered(1))
    return pl.pallas_call(
        _ffn_kernel,
        grid=(B, S // ROW_TILE),
        in_specs=[rows(D), rows(D), whole(w_up), whole(conv_w), whole(conv_b), whole(w_down), whole(g_final)],
        out_specs=rows(D),
        out_shape=jax.ShapeDtypeStruct((B, S, D), F32),
        scratch_shapes=[pltpu.VMEM((ROW_TILE, D_FF), BF16), pltpu.VMEM((SUBLANES, 2 * D_FF), F32)],
        compiler_params=pltpu.CompilerParams(
            dimension_semantics=("arbitrary", "arbitrary"), vmem_limit_bytes=VMEM_LIMIT),
        name="ffn",
    )(x1, h2, w_up, conv_w, conv_b, w_down, g_final)


def _in_proj_weight(w_in_l):
    def per_head(off_a, off_b):
        cols = []
        for h in range(N_HEADS):
            cols.append(w_in_l[:, off_a + h * HEAD_QK: off_a + (h + 1) * HEAD_QK])
            cols.append(w_in_l[:, off_b + h * HEAD_QK: off_b + (h + 1) * HEAD_QK])
        return cols
    cols = (per_head(_OFF_Q1, _OFF_Q2) + per_head(_OFF_K1, _OFF_K2) + per_head(_OFF_QR, _OFF_KR)
            + [w_in_l[:, _OFF_VR:_OFF_VR + HV], w_in_l[:, _OFF_VA:_OFF_VA + HV]])
    return jnp.concatenate(cols, axis=1).astype(BF16)


def kernel(x, g_mix, w_in, lam_q1, lam_k1, lam_q2, lam_k2, g_diff_sub, w_diff_proj, w_ret_proj, w_out, g_ffn,
           w_up, conv_w, conv_b, w_down, g_final):
    depth = w_in.shape[0]
    row = lambda v: v.reshape(1, -1)
    for l in range(depth):
        lam_init = 0.8 - 0.6 * math.exp(-0.3 * l)
        proj, vt = _in_proj(x, row(g_mix[l]), _in_proj_weight(w_in[l]))
        a = _diff_attn(proj, vt, row(lam_q1[l]), row(lam_k1[l]), row(lam_q2[l]), row(lam_k2[l]),
                       row(g_diff_sub[l]), lam_init)
        r = _retention(proj)
        x1, h2 = _merge(x, a, r, row(g_mix[l]), w_in[l][:, _OFF_GATES:].astype(BF16),
                        w_diff_proj[l].astype(BF16), w_ret_proj[l].astype(BF16), w_out[l].astype(BF16),
                        row(g_ffn[l]))
        assert depth == 1
        x = _ffn(x1, h2, w_up[l].astype(BF16), conv_w[l], row(conv_b[l]), w_down[l].astype(BF16), row(g_final))
    return x
```

```python
import functools
import math

import jax
import jax.numpy as jnp
from jax import lax
from jax.experimental import pallas as pl
from jax.experimental.pallas import tpu as pltpu

D_MODEL = 1024
N_HEADS = 4
HEAD_QK = 64
HEAD_V = 128
D_FF = 2816
CONV_WIDTH = 3
NORM_EPS = 1e-6
SUBLN_EPS = 1e-5

LANES = 128
SUBLANES = 8
VMEM_LIMIT = 56 * 1024 * 1024

ROW_TILE = 512
IN_PROJ_ROWS = 1024
RET_CHUNK = 256
FF_CHUNK = 256

_OFF_Q1, _OFF_Q2, _OFF_K1, _OFF_K2, _OFF_VA = 0, 256, 512, 768, 1024
_OFF_QR, _OFF_KR, _OFF_VR, _OFF_GATES = 1536, 1792, 2048, 2560
HV = N_HEADS * HEAD_V
_COL_QQ, _COL_KK, _COL_QKR, _COL_VR = 0, 1, 2, 3
N_PROJ_BLOCKS = 4

NEG_BIG = -1e30
LOG2E = math.log2(math.e)
Q_TILE = 1024
V_ROWS = HEAD_V + 2 * SUBLANES
BF16 = jnp.bfloat16
F32 = jnp.float32


def _rms(x, eps):
    return x * lax.rsqrt(jnp.mean(x * x, axis=-1, keepdims=True) + eps)


def _dot(a, b):
    return jnp.dot(a, b, preferred_element_type=F32)


def _dot_nt(a, b):
    return lax.dot_general(a, b, (((1,), (1,)), ((), ())), preferred_element_type=F32)


def _dot_tn(a, b):
    return lax.dot_general(a, b, (((0,), (0,)), ((), ())), preferred_element_type=F32)


def _in_proj_kernel(x_ref, g_ref, w_ref, o_ref, vt_ref):
    h = (_rms(x_ref[0], NORM_EPS) * g_ref[...]).astype(BF16)
    scales = {_COL_QQ: HEAD_QK ** -0.5 * LOG2E}
    for c in range(N_PROJ_BLOCKS):
        cols = slice(c * HV, (c + 1) * HV)
        y = _dot(h, w_ref[:, cols])
        if c in scales:
            y = y * scales[c]
        o_ref[0, :, cols] = y.astype(BF16)
    va = _dot(h, w_ref[:, N_PROJ_BLOCKS * HV:])
    for i in range(IN_PROJ_ROWS // ROW_TILE):
        vt_ref[0, i] = va[i * ROW_TILE:(i + 1) * ROW_TILE].T.astype(BF16)


def _in_proj(x, g, w):
    B, S, D = x.shape
    n_tiles = S // ROW_TILE
    n_cols = N_PROJ_BLOCKS * HV
    per_step = IN_PROJ_ROWS // ROW_TILE
    assert S % IN_PROJ_ROWS == 0 and IN_PROJ_ROWS % ROW_TILE == 0
    return pl.pallas_call(
        _in_proj_kernel,
        grid=(B, S // IN_PROJ_ROWS),
        in_specs=[
            pl.BlockSpec((1, IN_PROJ_ROWS, D), lambda b, i: (b, i, 0)),
            pl.BlockSpec((1, D), lambda b, i: (0, 0)),
            pl.BlockSpec((D, n_cols + HV), lambda b, i: (0, 0)),
        ],
        out_specs=[
            pl.BlockSpec((1, IN_PROJ_ROWS, n_cols), lambda b, i: (b, i, 0)),
            pl.BlockSpec((1, per_step, HV, ROW_TILE), lambda b, i: (b, i, 0, 0)),
        ],
        out_shape=[
            jax.ShapeDtypeStruct((B, S, n_cols), BF16),
            jax.ShapeDtypeStruct((B, n_tiles, HV, ROW_TILE), BF16),
        ],
        compiler_params=pltpu.CompilerParams(
            dimension_semantics=("arbitrary", "arbitrary"), vmem_limit_bytes=VMEM_LIMIT),
        name="in_proj",
    )(x, g, w)


def _diff_attn_kernel(lq1_ref, lk1_ref, lq2_ref, lk2_ref, gsub_ref, qq_ref, kk_ref, vt_ref, o_ref,
                      q1_ref, q2_ref, k1_ref, k2_ref, vta_ref, s1a_ref, s1b_ref, s2a_ref, s2b_ref,
                      mx1a_ref, mx1b_ref, mx2a_ref, mx2b_ref, acc1_ref, acc2_ref, m1_ref, m2_ref, *, lam_init):
    T = ROW_TILE
    TQ = Q_TILE
    SUBS = TQ // T
    n_tiles = k1_ref.shape[0] // T
    n_q = k1_ref.shape[0] // TQ
    h = pl.program_id(1)
    slope = jnp.where(h == 0, 2.0 ** -2, jnp.where(h == 1, 2.0 ** -4, jnp.where(h == 2, 2.0 ** -6, 2.0 ** -8)))
    slope = slope.astype(F32) * LOG2E

    lane = lax.broadcasted_iota(jnp.int32, (T, LANES), 1)
    is_qk = lane < HEAD_QK
    bias_lanes = [lane == HEAD_QK + i for i in range(3)]
    bias = lax.broadcasted_iota(jnp.int32, (T, LANES), 0).astype(F32) * slope
    hi = bias.astype(BF16).astype(F32)
    mid = (bias - hi).astype(BF16).astype(F32)
    lo = (bias - hi - mid).astype(BF16).astype(F32)
    bias_cols = jnp.where(bias_lanes[0], hi, jnp.where(bias_lanes[1], mid, jnp.where(bias_lanes[2], lo, 0.0)))
    one_cols = jnp.where(bias_lanes[0] | bias_lanes[1] | bias_lanes[2], 1.0, 0.0)
    sub_idx = lax.broadcasted_iota(jnp.int32, (2 * SUBLANES, T), 0)
    ones_rows = jnp.where(sub_idx == 0, 1.0, 0.0).astype(BF16)

    def prepare(t, carry):
        rows = pl.ds(pl.multiple_of(t * T, T), T)
        kk = kk_ref[0, rows, :].astype(F32)
        k1_ref[rows, :] = jnp.where(is_qk, kk, bias_cols).astype(BF16)
        k2_ref[rows, :] = jnp.where(is_qk, pltpu.roll(kk, HEAD_QK, 1), bias_cols).astype(BF16)
        qq = qq_ref[0, rows, :].astype(F32)
        q1_ref[rows, :] = jnp.where(is_qk, qq, one_cols).astype(BF16)
        q2_ref[rows, :] = jnp.where(is_qk, pltpu.roll(qq, HEAD_QK, 1), one_cols).astype(BF16)
        vta_ref[t, :HEAD_V, :] = vt_ref[0, t]
        vta_ref[t, HEAD_V:, :] = ones_rows
        return carry

    lax.fori_loop(0, n_tiles, prepare, 0)

    streams = ((q1_ref, k1_ref, (s1a_ref, s1b_ref), (mx1a_ref, mx1b_ref), m1_ref, acc1_ref),
               (q2_ref, k2_ref, (s2a_ref, s2b_ref), (mx2a_ref, mx2b_ref), m2_ref, acc2_ref))
    full = slice(0, TQ)

    def scores(stream, qi, j, slot, sub, cols=full, masked=False):
        q_ref, k_ref, s_ref, mx_ref, _, _ = stream
        q_rows = pl.ds(pl.multiple_of(qi * TQ, TQ) + cols.start, cols.stop - cols.start)
        k_rows = pl.ds(pl.multiple_of(j * T, T), T)
        s = _dot_nt(k_ref[k_rows, :], q_ref[q_rows, :])
        if masked:
            key_idx = lax.broadcasted_iota(jnp.int32, s.shape, 0)
            qry_idx = lax.broadcasted_iota(jnp.int32, s.shape, 1)
            s = jnp.where(key_idx <= qry_idx, s, NEG_BIG)
        s_ref[slot][sub, :, cols] = s
        mx_ref[slot][sub, :, cols] = jnp.max(s, axis=0, keepdims=True)

    def update(stream, qi, items, cols, score_jobs):
        _, _, s_ref, mx_ref, m_ref, acc_ref = stream
        offsets = [slope * jnp.asarray(j * T - qi * TQ, F32) for j, _, _ in items]
        m_old = m_ref[:, cols]
        m_new = m_old
        for (j, slot, sub), offset in zip(items, offsets):
            m_new = jnp.maximum(m_new, mx_ref[slot][sub, :, cols] + offset)
        alpha = jnp.exp2(m_old - m_new)
        m_ref[:, cols] = m_new
        pv = None
        for (j, slot, sub), offset in zip(items, offsets):
            if score_jobs:
                score_jobs.pop(0)()
            p = jnp.exp2((s_ref[slot][sub, :, cols] - (m_new - offset)).astype(BF16))
            d = _dot(vta_ref[j], p)
            pv = d if pv is None else pv + d
        acc_ref[:, cols] = alpha * acc_ref[:, cols] + pv

    def reset():
        for _, _, _, _, m_ref, acc_ref in streams:
            m_ref[...] = jnp.full(m_ref.shape, NEG_BIG, F32)
            acc_ref[...] = jnp.zeros(acc_ref.shape, F32)

    lam = (jnp.exp(jnp.sum(lq1_ref[...] * lk1_ref[...], axis=-1, keepdims=True))
           - jnp.exp(jnp.sum(lq2_ref[...] * lk2_ref[...], axis=-1, keepdims=True)) + lam_init)
    out_gain = gsub_ref[...] * (1.0 - lam_init)

    def finish(qi):
        o1 = acc1_ref[:HEAD_V, :] * (1.0 / acc1_ref[HEAD_V:HEAD_V + 1, :])
        o2 = acc2_ref[:HEAD_V, :] * (1.0 / acc2_ref[HEAD_V:HEAD_V + 1, :])
        a = o1 - lam * o2
        a = a * lax.rsqrt(jnp.mean(a * a, axis=0, keepdims=True) + SUBLN_EPS)
        o_ref[0, pl.ds(pl.multiple_of(qi * TQ, TQ), TQ), :] = (a.T * out_gain).astype(BF16)

    def scores_full(qi, step, slot):
        return lambda stream: [functools.partial(scores, stream, qi, step * SUBS + sub, slot, sub) for sub in range(SUBS)]

    def full_items(step, slot):
        return [(step * SUBS + sub, slot, sub) for sub in range(SUBS)]

    def scores_diagonal(qi, slot):
        return lambda stream: [functools.partial(scores, stream, qi, qi * SUBS + sub, slot, sub,
                                                 cols=slice(sub * T, TQ), masked=True) for sub in range(SUBS)]

    def diagonal_updates(qi, slot):
        return [([(qi * SUBS + sub, slot, sub) for sub in range(c + 1)], slice(c * T, (c + 1) * T)) for c in range(SUBS)]

    def stage(qi, updates, score_jobs_of):
        for stream in streams:
            jobs = score_jobs_of(stream)
            for items, cols in updates:
                update(stream, qi, items, cols, jobs)
            for job in jobs:
                job()

    reset()
    for stream in streams:
        for job in scores_diagonal(0, 0)(stream):
            job()
    stage(0, diagonal_updates(0, 0), scores_full(min(1, n_q - 1), 0, 1))
    finish(0)

    def by_parity(value, fn):
        for parity in range(2):
            pl.when(value % 2 == parity)(functools.partial(fn, parity))

    def q_tile(qi, slot0):
        reset()

        def step(t, carry):
            def run(cur):
                stage(qi, [(full_items(t, cur), full)], scores_full(qi, t + 1, 1 - cur))
            by_parity(slot0 + t, run)
            return carry

        lax.fori_loop(0, qi - 1, step, 0)

        def tail(cur):
            stage(qi, [(full_items(qi - 1, cur), full)], scores_diagonal(qi, 1 - cur))
            stage(qi, diagonal_updates(qi, 1 - cur), scores_full(jnp.minimum(qi + 1, n_q - 1), 0, cur))
            finish(qi)
        by_parity(slot0 + qi - 1, tail)
        return (slot0 + qi + 1) % 2

    lax.fori_loop(1, n_q, q_tile, jnp.int32(1))


def _diff_attn(proj, vt, lam_q1, lam_k1, lam_q2, lam_k2, g_sub, lam_init):
    B, S, _ = proj.shape
    T = ROW_TILE
    n_tiles = S // T
    TQ = Q_TILE
    subs = TQ // T
    assert S % TQ == 0 and TQ % T == 0
    small = lambda n: pl.BlockSpec((1, n), lambda b, h: (0, 0))
    seq = lambda: pltpu.VMEM((S, LANES), BF16)
    return pl.pallas_call(
        functools.partial(_diff_attn_kernel, lam_init=lam_init),
        grid=(B, N_HEADS),
        in_specs=[
            small(HEAD_QK), small(HEAD_QK), small(HEAD_QK), small(HEAD_QK), small(HEAD_V),
            pl.BlockSpec((1, S, LANES), lambda b, h: (b, 0, _COL_QQ * N_HEADS + h)),
            pl.BlockSpec((1, S, LANES), lambda b, h: (b, 0, _COL_KK * N_HEADS + h)),
            pl.BlockSpec((1, n_tiles, HEAD_V, T), lambda b, h: (b, 0, h, 0)),
        ],
        out_specs=pl.BlockSpec((1, S, HEAD_V), lambda b, h: (b, 0, h)),
        out_shape=jax.ShapeDtypeStruct((B, S, HV), BF16),
        scratch_shapes=[
            seq(), seq(), seq(), seq(),
            pltpu.VMEM((n_tiles, V_ROWS, T), BF16),
            *[pltpu.VMEM((subs, T, TQ), F32)] * 4,
            *[pltpu.VMEM((subs, 1, TQ), F32)] * 4,
            pltpu.VMEM((V_ROWS, TQ), F32), pltpu.VMEM((V_ROWS, TQ), F32),
            pltpu.VMEM((1, TQ), F32), pltpu.VMEM((1, TQ), F32),
        ],
        compiler_params=pltpu.CompilerParams(
            dimension_semantics=("arbitrary", "arbitrary"), vmem_limit_bytes=VMEM_LIMIT),
        name="diff_attn",
    )(lam_q1, lam_k1, lam_q2, lam_k2, g_sub, proj, proj, vt)


def _merge_kernel(x_ref, a_ref, qk_ref, v_ref, gmix_ref, wg_ref, wdp_ref, wrp_ref, wout_ref, gffn_ref, x1_ref, h2_ref,
                  state_ref, decay_ref, qdec_ref, kdec_ref):
    C = RET_CHUNK
    n_chunks = ROW_TILE // C
    log_g = [math.log(1.0 - 2.0 ** (-5.0 - h)) for h in range(N_HEADS)]

    @pl.when((pl.program_id(0) == 0) & (pl.program_id(1) == 0))
    def _():
        row = lax.broadcasted_iota(jnp.int32, (C, C), 0)
        col = lax.broadcasted_iota(jnp.int32, (C, C), 1)
        rel = (row - col).astype(F32)
        pos = lax.broadcasted_iota(jnp.int32, (C, LANES), 0).astype(F32)
        for h in range(N_HEADS):
            decay_ref[h] = jnp.where(rel >= 0, jnp.exp(log_g[h] * jnp.maximum(rel, 0.0)), 0.0)
            qdec_ref[h] = jnp.exp(log_g[h] * (pos + 1.0))
            kdec_ref[h] = jnp.exp(log_g[h] * (C - 1.0 - pos))

    @pl.when(pl.program_id(1) == 0)
    def _():
        state_ref[...] = jnp.zeros(state_ref.shape, F32)

    is_q = lax.broadcasted_iota(jnp.int32, (C, LANES), 1) < HEAD_QK

    def retention_scores(c):
        parts = []
        rows = slice(c * C, (c + 1) * C)
        for h in range(N_HEADS):
            cols = slice(h * LANES, (h + 1) * LANES)
            qk = qk_ref[0, rows, cols].astype(F32)
            q = jnp.where(is_q, qk, 0.0)
            k = jnp.where(is_q, pltpu.roll(qk, HEAD_QK, 1), 0.0) * HEAD_QK ** -0.5
            v = v_ref[0, rows, cols]
            scores = (_dot_nt(q.astype(BF16), k.astype(BF16)) * decay_ref[h]).astype(BF16)
            parts.append((q, k, v, scores))
        return parts

    def retention_output(parts):
        outs = []
        for h, (q, k, v, scores) in enumerate(parts):
            state = state_ref[h]
            o = _dot(scores, v) + _dot((q * qdec_ref[h]).astype(BF16), state.astype(BF16))
            state_ref[h] = state * math.exp(log_g[h] * C) + _dot_tn((k * kdec_ref[h]).astype(BF16), v)
            outs.append(_rms(o, SUBLN_EPS))
        return jnp.concatenate(outs, axis=1)

    x = x_ref[0]
    h = (_rms(x, NORM_EPS) * gmix_ref[...]).astype(BF16)
    large_dots = [lambda: _dot(h, wg_ref[:, :HV]), lambda: _dot(h, wg_ref[:, HV:HV + D_MODEL]),
                  lambda: _dot(h, wg_ref[:, HV + D_MODEL:]), lambda: _dot(a_ref[0], wdp_ref[...])]
    large = []
    ret = []
    for c in range(n_chunks):
        parts = retention_scores(c)
        if large_dots:
            large.append(large_dots.pop(0)())
        ret.append(retention_output(parts))
        if large_dots:
            large.append(large_dots.pop(0)())
    large += [dot() for dot in large_dots]
    swish_gate, gate_a, gate_r, diff_proj = large
    r = (jnp.concatenate(ret, axis=0) * jax.nn.silu(swish_gate)).astype(BF16)
    merged = jax.nn.sigmoid(gate_a) * diff_proj + jax.nn.sigmoid(gate_r) * _dot(r, wrp_ref[...])
    x1 = x + _dot(merged.astype(BF16), wout_ref[...])
    x1_ref[0] = x1
    h2_ref[0] = (_rms(x1, NORM_EPS) * gffn_ref[...]).astype(BF16)


def _merge(x, a, proj, g_mix, w_gates, w_diff_proj, w_ret_proj, w_out, g_ffn):
    B, S, D = x.shape
    C = RET_CHUNK
    assert ROW_TILE % C == 0
    rows = lambda n: pl.BlockSpec((1, ROW_TILE, n), lambda b, i: (b, i, 0))
    proj_cols = lambda c: pl.BlockSpec((1, ROW_TILE, HV), lambda b, i: (b, i, c))
    whole = lambda arr: pl.BlockSpec(arr.shape, lambda b, i: (0,) * arr.ndim)
    return pl.pallas_call(
        _merge_kernel,
        grid=(B, S // ROW_TILE),
        in_specs=[rows(D), rows(HV), proj_cols(_COL_QKR), proj_cols(_COL_VR), whole(g_mix), whole(w_gates),
                  whole(w_diff_proj), whole(w_ret_proj), whole(w_out), whole(g_ffn)],
        out_specs=[rows(D), rows(D)],
        out_shape=[jax.ShapeDtypeStruct((B, S, D), F32), jax.ShapeDtypeStruct((B, S, D), BF16)],
        scratch_shapes=[
            pltpu.VMEM((N_HEADS, LANES, HEAD_V), F32),
            pltpu.VMEM((N_HEADS, C, C), F32),
            pltpu.VMEM((N_HEADS, C, LANES), F32),
            pltpu.VMEM((N_HEADS, C, LANES), F32),
        ],
        compiler_params=pltpu.CompilerParams(
            dimension_semantics=("arbitrary", "arbitrary"), vmem_limit_bytes=VMEM_LIMIT),
        name="merge",
    )(x, a, proj, proj, g_mix, w_gates, w_diff_proj, w_ret_proj, w_out, g_ffn)


def _ffn_kernel(x1_ref, h2_ref, wup_ref, cw_ref, cb_ref, wdown_ref, gfin_ref, o_ref, act_ref, carry_ref):
    T = ROW_TILE
    first_tile = pl.program_id(1) == 0

    @pl.when(first_tile)
    def _():
        carry_ref[...] = jnp.zeros(carry_ref.shape, F32)

    h2 = h2_ref[0]
    row = lax.broadcasted_iota(jnp.int32, (T, FF_CHUNK), 0)

    def conv(cols):
        u = _dot(h2, wup_ref[:, cols])
        prev = carry_ref[:, cols]
        prev1 = prev[SUBLANES - 1:SUBLANES, :]
        prev2 = prev[SUBLANES - 2:SUBLANES - 1, :]
        u1 = jnp.where(row == 0, prev1, pltpu.roll(u, 1, 0))
        u2 = jnp.where(row == 0, prev2, jnp.where(row == 1, prev1, pltpu.roll(u, 2, 0)))
        carry_ref[:, cols] = u[T - SUBLANES:, :]
        w = cw_ref[:, cols]
        return cb_ref[:, cols] + u * w[2:3, :] + u1 * w[1:2, :] + u2 * w[0:1, :]

    for f in range(D_FF // FF_CHUNK):
        zg = conv(slice(f * FF_CHUNK, (f + 1) * FF_CHUNK))
        zu = conv(slice(D_FF + f * FF_CHUNK, D_FF + (f + 1) * FF_CHUNK))
        act_ref[:, f * FF_CHUNK:(f + 1) * FF_CHUNK] = (jax.nn.silu(zg) * zu).astype(BF16)

    y = x1_ref[0] + _dot(act_ref[...], wdown_ref[...])
    o_ref[0] = _rms(y, NORM_EPS) * gfin_ref[...]


def _ffn(x1, h2, w_up, conv_w, conv_b, w_down, g_final):
    B, S, D = x1.shape
    rows = lambda n: pl.BlockSpec((1, ROW_TILE, n), lambda b, i: (b, i, 0))
    whole = lambda arr: pl.BlockSpec(arr.shape, lambda b, i: (0,) * arr.ndim, pipeline_mode=pl.Buffered(1))
    return pl.pallas_call(
        _ffn_kernel,
        grid=(B, S // ROW_TILE),
        in_specs=[rows(D), rows(D), whole(w_up), whole(conv_w), whole(conv_b), whole(w_down), whole(g_final)],
        out_specs=rows(D),
        out_shape=jax.ShapeDtypeStruct((B, S, D), F32),
        scratch_shapes=[pltpu.VMEM((ROW_TILE, D_FF), BF16), pltpu.VMEM((SUBLANES, 2 * D_FF), F32)],
        compiler_params=pltpu.CompilerParams(
            dimension_semantics=("arbitrary", "arbitrary"), vmem_limit_bytes=VMEM_LIMIT),
        name="ffn",
    )(x1, h2, w_up, conv_w, conv_b, w_down, g_final)


def _in_proj_weight(w_in_l):
    def per_head(off_a, off_b):
        cols = []
        for h in range(N_HEADS):
            cols.append(w_in_l[:, off_a + h * HEAD_QK: off_a + (h + 1) * HEAD_QK])
            cols.append(w_in_l[:, off_b + h * HEAD_QK: off_b + (h + 1) * HEAD_QK])
        return cols
    cols = (per_head(_OFF_Q1, _OFF_Q2) + per_head(_OFF_K1, _OFF_K2) + per_head(_OFF_QR, _OFF_KR)
            + [w_in_l[:, _OFF_VR:_OFF_VR + HV], w_in_l[:, _OFF_VA:_OFF_VA + HV]])
    return jnp.concatenate(cols, axis=1).astype(BF16)


def kernel(x, g_mix, w_in, lam_q1, lam_k1, lam_q2, lam_k2, g_diff_sub, w_diff_proj, w_ret_proj, w_out, g_ffn,
           w_up, conv_w, conv_b, w_down, g_final):
    depth = w_in.shape[0]
    row = lambda v: v.reshape(1, -1)
    for l in range(depth):
        lam_init = 0.8 - 0.6 * math.exp(-0.3 * l)
        proj, vt = _in_proj(x, row(g_mix[l]), _in_proj_weight(w_in[l]))
        a = _diff_attn(proj, vt, row(lam_q1[l]), row(lam_k1[l]), row(lam_q2[l]), row(lam_k2[l]),
                       row(g_diff_sub[l]), lam_init)
        x1, h2 = _merge(x, a, proj, row(g_mix[l]), w_in[l][:, _OFF_GATES:].astype(BF16),
                        w_diff_proj[l].astype(BF16), w_ret_proj[l].astype(BF16), w_out[l].astype(BF16),
                        row(g_ffn[l]))
        assert depth == 1
        x = _ffn(x1, h2, w_up[l].astype(BF16), conv_w[l], row(conv_b[l]), w_down[l].astype(BF16), row(g_final))
    return x
```
